```python
import jax, jax.numpy as jnp
from jax import lax
import numpy as np

D_MODEL = 2048
BATCH = 4
SEQ = 4096
DEPTH = 4

GRID_W = 64
CTX_LEN = 256
N_HEADS = 16
N_KV_HEADS = 4
HEAD_DIM = 128
Q_GROUPS = N_HEADS // N_KV_HEADS
ATTN_W = N_HEADS * HEAD_DIM
KV_W = N_KV_HEADS * HEAD_DIM
AXIS_DIM = HEAD_DIM // 2
ROPE_THETA = 10000.0
ATTN_SCALE = HEAD_DIM ** -0.5
Q_BLOCK = 128
LRU_W = D_MODEL
LRU_BLOCKS = 16
LRU_BW = LRU_W // LRU_BLOCKS
CONV_W = 4
CONV_PAD_L = 2
RG_C = 8.0
CTX_COLS = 2 * KV_W + LRU_W
Q_END = CTX_COLS + ATTN_W
XG_END = Q_END + LRU_W
IN_COLS = XG_END + 2 * D_MODEL
PEER_HEADS = 8
PEER_DK = 128
N_KEYS = 128
N_EXPERTS = N_KEYS * N_KEYS
PEER_TOPK = 16
PEER_BLOCK = 128
DEEPNORM_ALPHA = (2 * DEPTH) ** 0.25
DEEPNORM_BETA = (8 * DEPTH) ** -0.25
LN_EPS = 1e-6

kernel_name = 'hybrid_rglru_gqa_peer_flow_block'


def layer_norm(x):
    xf = x.astype(jnp.float32)
    mu = jnp.mean(xf, axis=-1, keepdims=True)
    var = jnp.mean(jnp.square(xf - mu), axis=-1, keepdims=True)
    return ((xf - mu) * lax.rsqrt(var + LN_EPS)).astype(x.dtype)


def rms_norm(x, g):
    xf = x.astype(jnp.float32)
    return (xf * lax.rsqrt(jnp.mean(jnp.square(xf), axis=-1, keepdims=True) + LN_EPS)).astype(x.dtype) * g


def modulate(x, shift, scale):
    return layer_norm(x) * (1 + scale) + shift


def post_norm(res, delta, g, b):
    return layer_norm(DEEPNORM_ALPHA * res + delta) * g + b


def ada_mods(cond, w, b):
    m = (jax.nn.silu(cond) @ w + b).reshape(cond.shape[:-1] + (6, D_MODEL))
    return tuple(jnp.expand_dims(m[..., i, :], -2) for i in range(6))


def rope_tables_2d(rows, dtype):
    row = jnp.repeat(jnp.arange(rows, dtype=jnp.float32), GRID_W)
    col = jnp.tile(jnp.arange(GRID_W, dtype=jnp.float32), rows)
    inv_freq = ROPE_THETA ** (-jnp.arange(0, AXIS_DIM, 2, dtype=jnp.float32) / AXIS_DIM)
    ang = jnp.stack([row[:, None] * inv_freq, col[:, None] * inv_freq], axis=1)
    ang = ang[:, None, :, None, :]
    return jnp.cos(ang).astype(dtype), jnp.sin(ang).astype(dtype)


def apply_rope_2d(x, cos, sin):
    B, T, H, _ = x.shape
    xs = x.reshape(B, T, H, 2, 2, AXIS_DIM // 2)
    rot = jnp.stack([-xs[..., 1, :], xs[..., 0, :]], axis=-2)
    return (xs * cos + rot * sin).reshape(B, T, H, HEAD_DIM)


def attend(q, k, v):
    B, T = q.shape[0], q.shape[1]
    qb = q.reshape(B, T // Q_BLOCK, Q_BLOCK, N_KV_HEADS, Q_GROUPS, HEAD_DIM).transpose(1, 0, 2, 3, 4, 5)

    def block(qi):
        s = jnp.einsum('bqkgd,bskd->bkgqs', qi, k, preferred_element_type=jnp.float32) * ATTN_SCALE
        p = jax.nn.softmax(s, axis=-1).astype(v.dtype)
        return jnp.einsum('bkgqs,bskd->bqkgd', p, v)

    o = lax.map(block, qb)
    return o.transpose(1, 0, 2, 3, 4, 5).reshape(B, T, ATTN_W)


def centred_dwconv(x, w, b):
    T = x.shape[1]
    xp = jnp.pad(x, ((0, 0), (CONV_PAD_L, CONV_W - 1 - CONV_PAD_L), (0, 0)))
    acc = xp[:, 0:T] * w[0]
    for j in range(1, CONV_W):
        acc = acc + xp[:, j:j + T] * w[j]
    return acc + b


def block_diag(x, w, b):
    B, T, _ = x.shape
    xb = x.reshape(B, T, LRU_BLOCKS, LRU_BW)
    return (jnp.einsum('btnj,njk->btnk', xb, w) + b).reshape(B, T, LRU_W)


def linear_scan(a, u, h0, reverse):
    first = -1 if reverse else 0
    u = u.at[:, first].add(a[:, first] * h0)

    def combine(l, r):
        a_l, b_l = l
        a_r, b_r = r
        return a_l * a_r, a_r * b_l + b_r

    _, h = lax.associative_scan(combine, (a, u), reverse=reverse, axis=1)
    return h


def rglru_scan(xc, dir_params, h0, reverse):
    w_a, b_a, w_x, b_x, lam = dir_params
    r = jax.nn.sigmoid(block_diag(xc, w_a, b_a)).astype(jnp.float32)
    i = jax.nn.sigmoid(block_diag(xc, w_x, b_x)).astype(jnp.float32)
    log_a = -RG_C * r * jax.nn.softplus(-lam.astype(jnp.float32))
    a = jnp.exp(log_a)
    u = jnp.sqrt(-jnp.expm1(2.0 * log_a)) * i * xc.astype(jnp.float32)
    return linear_scan(a, u, h0, reverse)


def merge_branches(attn_o, lru_y, xg, gate_logits, w_attn_br, w_lru_br, w_out):
    attn_b = attn_o @ w_attn_br
    lru_b = (lru_y * jax.nn.gelu(xg)) @ w_lru_br
    g_attn, g_lru = jnp.split(jax.nn.sigmoid(gate_logits), 2, axis=-1)
    return (g_attn * attn_b + g_lru * lru_b) @ w_out


def peer_ffn(h, w_pq, sub_k1, sub_k2, peer_u, peer_v):
    B, T, D = h.shape
    q = (h @ w_pq).reshape(B, T, PEER_HEADS, 2, PEER_DK)
    s1 = jnp.einsum('bthd,hkd->bthk', q[..., 0, :], sub_k1, preferred_element_type=jnp.float32)
    s2 = jnp.einsum('bthd,hkd->bthk', q[..., 1, :], sub_k2, preferred_element_type=jnp.float32)
    v1, i1 = lax.top_k(s1, PEER_TOPK)
    v2, i2 = lax.top_k(s2, PEER_TOPK)
    cand = (v1[..., :, None] + v2[..., None, :]).reshape(B, T, PEER_HEADS, PEER_TOPK * PEER_TOPK)
    vs, ci = lax.top_k(cand, PEER_TOPK)
    idx = (jnp.take_along_axis(i1, ci // PEER_TOPK, axis=-1) * N_KEYS
           + jnp.take_along_axis(i2, ci % PEER_TOPK, axis=-1))
    gates = jax.nn.softmax(vs, axis=-1).astype(h.dtype)
    nb = (B * T) // PEER_BLOCK
    hb = h.reshape(nb, PEER_BLOCK, D)
    ib = idx.reshape(nb, PEER_BLOCK, PEER_HEADS, PEER_TOPK)
    gb = gates.reshape(nb, PEER_BLOCK, PEER_HEADS, PEER_TOPK)

    def block(args):
        ht, it, gt = args
        act = jax.nn.gelu(jnp.einsum('thkd,td->thk', peer_u[it], ht)) * gt
        return jnp.einsum('thk,thkd->td', act, peer_v[it])

    return lax.map(block, (hb, ib, gb)).reshape(B, T, D)


def setup_inputs(seed: int = 0) -> dict:
    key = jax.random.key(seed)
    ks = iter(jax.random.split(key, 48))
    L, D = DEPTH, D_MODEL

    def nrm(shape, scale):
        return jax.random.normal(next(ks), shape, jnp.float32) * scale

    def lam_init():
        u = jax.random.uniform(next(ks), (L, LRU_W), jnp.float32, 0.9, 0.999)
        a = u ** (1.0 / RG_C)
        return jnp.log(a) - jnp.log1p(-a)

    return {
        'x': nrm((BATCH, SEQ, D), 1.0),
        'c': nrm((BATCH, D), 1.0),
        'ctx': nrm((BATCH, CTX_LEN, D), 1.0),
        'c_ctx': nrm((D,), 1.0),
        'w_ada': nrm((L, D, 6 * D), 0.5 * D ** -0.5),
        'b_ada': nrm((L, 6 * D), 0.02),
        'w_in': nrm((L, D, IN_COLS), D ** -0.5),
        'q_gain': 1.0 + nrm((L, HEAD_DIM), 0.02),
        'k_gain': 1.0 + nrm((L, HEAD_DIM), 0.02),
        'conv_w': nrm((L, CONV_W, LRU_W), CONV_W ** -0.5),
        'conv_b': nrm((L, LRU_W), 0.02),
        'wa_f': nrm((L, LRU_BLOCKS, LRU_BW, LRU_BW), LRU_BW ** -0.5),
        'ba_f': nrm((L, LRU_BLOCKS, LRU_BW), 0.02),
        'wx_f': nrm((L, LRU_BLOCKS, LRU_BW, LRU_BW), LRU_BW ** -0.5),
        'bx_f': nrm((L, LRU_BLOCKS, LRU_BW), 0.02),
        'lam_f': lam_init(),
        'wa_b': nrm((L, LRU_BLOCKS, LRU_BW, LRU_BW), LRU_BW ** -0.5),
        'ba_b': nrm((L, LRU_BLOCKS, LRU_BW), 0.02),
        'wx_b': nrm((L, LRU_BLOCKS, LRU_BW, LRU_BW), LRU_BW ** -0.5),
        'bx_b': nrm((L, LRU_BLOCKS, LRU_BW), 0.02),
        'lam_b': lam_init(),
        'w_attn_br': nrm((L, ATTN_W, D), DEEPNORM_BETA * ATTN_W ** -0.5),
        'w_lru_br': nrm((L, LRU_W, D), DEEPNORM_BETA * LRU_W ** -0.5),
        'w_out': nrm((L, D, D), DEEPNORM_BETA * D ** -0.5),
        'ln1_g': 1.0 + nrm((L, D), 0.02),
        'ln1_b': nrm((L, D), 0.02),
        'w_pq': nrm((L, D, PEER_HEADS * 2 * PEER_DK), D ** -0.5),
        'sub_k1': nrm((L, PEER_HEADS, N_KEYS, PEER_DK), PEER_DK ** -0.5),
        'sub_k2': nrm((L, PEER_HEADS, N_KEYS, PEER_DK), PEER_DK ** -0.5),
        'peer_u': nrm((L, N_EXPERTS, D), D ** -0.5),
        'peer_v': nrm((L, N_EXPERTS, D), DEEPNORM_BETA * PEER_HEADS ** -0.5),
        'ln2_g': 1.0 + nrm((L, D), 0.02),
        'ln2_b': nrm((L, D), 0.02),
    }


def reference(x, c, ctx, c_ctx, w_ada, b_ada, w_in, q_gain, k_gain, conv_w, conv_b,
              wa_f, ba_f, wx_f, bx_f, lam_f, wa_b, ba_b, wx_b, bx_b, lam_b,
              w_attn_br, w_lru_br, w_out, ln1_g, ln1_b,
              w_pq, sub_k1, sub_k2, peer_u, peer_v, ln2_g, ln2_b):
    B, S, _ = x.shape
    C = ctx.shape[1]
    ROWS = S // GRID_W
    cos, sin = rope_tables_2d(ROWS, x.dtype)
    for l in range(DEPTH):
        last = l == DEPTH - 1
        sh1, sc1, g1, sh2, sc2, g2 = ada_mods(c, w_ada[l], b_ada[l])
        sh1c, sc1c, g1c, sh2c, sc2c, g2c = ada_mods(c_ctx, w_ada[l], b_ada[l])
        fwd = (wa_f[l], ba_f[l], wx_f[l], bx_f[l], lam_f[l])
        bwd = (wa_b[l], ba_b[l], wx_b[l], bx_b[l], lam_b[l])

        hc = modulate(ctx, sh1c, sc1c)
        pc = hc @ (w_in[l, :, :CTX_COLS] if last else w_in[l])
        kc = rms_norm(pc[..., :KV_W].reshape(B, C, N_KV_HEADS, HEAD_DIM), k_gain[l])
        vc = pc[..., KV_W:2 * KV_W].reshape(B, C, N_KV_HEADS, HEAD_DIM)
        xcc = centred_dwconv(pc[..., 2 * KV_W:CTX_COLS], conv_w[l], conv_b[l])
        h_zero = jnp.zeros((B, LRU_W), jnp.float32)
        hcf = rglru_scan(xcc, fwd, h_zero, False)
        hcb = rglru_scan(xcc, bwd, h_zero, True)

        hx = modulate(x, sh1, sc1)
        px = hx @ w_in[l]
        kx = rms_norm(px[..., :KV_W].reshape(B, S, N_KV_HEADS, HEAD_DIM), k_gain[l])
        vx = px[..., KV_W:2 * KV_W].reshape(B, S, N_KV_HEADS, HEAD_DIM)
        qx = rms_norm(px[..., CTX_COLS:Q_END].reshape(B, S, N_HEADS, HEAD_DIM), q_gain[l])
        qx = apply_rope_2d(qx, cos, sin)
        kx = apply_rope_2d(kx, cos, sin)
        attn_x = attend(qx, jnp.concatenate([kc, kx], axis=1), jnp.concatenate([vc, vx], axis=1))
        xcx = centred_dwconv(px[..., 2 * KV_W:CTX_COLS], conv_w[l], conv_b[l])
        lru_x = (rglru_scan(xcx, fwd, hcf[:, -1], False)
                 + rglru_scan(xcx, bwd, hcb[:, 0], True)).astype(x.dtype)
        mix_x = merge_branches(attn_x, lru_x, px[..., Q_END:XG_END], px[..., XG_END:],
                               w_attn_br[l], w_lru_br[l], w_out[l])
        x = post_norm(x, g1 * mix_x, ln1_g[l], ln1_b[l])
        ffn_x = peer_ffn(modulate(x, sh2, sc2), w_pq[l], sub_k1[l], sub_k2[l], peer_u[l], peer_v[l])
        x = post_norm(x, g2 * ffn_x, ln2_g[l], ln2_b[l])

        if not last:
            qc = rms_norm(pc[..., CTX_COLS:Q_END].reshape(B, C, N_HEADS, HEAD_DIM), q_gain[l])
            attn_c = attend(qc, kc, vc)
            lru_c = (hcf + hcb).astype(ctx.dtype)
            mix_c = merge_branches(attn_c, lru_c, pc[..., Q_END:XG_END], pc[..., XG_END:],
                                   w_attn_br[l], w_lru_br[l], w_out[l])
            ctx = post_norm(ctx, g1c * mix_c, ln1_g[l], ln1_b[l])
            ffn_c = peer_ffn(modulate(ctx, sh2c, sc2c), w_pq[l], sub_k1[l], sub_k2[l], peer_u[l], peer_v[l])
            ctx = post_norm(ctx, g2c * ffn_c, ln2_g[l], ln2_b[l])
    return x
```

```python
import functools
import math

import jax
import jax.numpy as jnp
from jax import lax
from jax.experimental import pallas as pl
from jax.experimental.pallas import tpu as pltpu

GRID_W = 64
ROPE_THETA = 10000.0
RG_C = 8.0
CONV_PAD_L = 2
PEER_TOPK = 16
LN_EPS = 1e-6

LANE = 128
SUBLANE = 8
VMEM_LIMIT = 56 * 1024 * 1024

F32 = jnp.float32
BF16 = jnp.bfloat16


def _cp(sem, vmem=VMEM_LIMIT):
    return pltpu.CompilerParams(dimension_semantics=sem, vmem_limit_bytes=vmem)


def _ln(x):
    mu = jnp.mean(x, axis=-1, keepdims=True)
    xc = x - mu
    var = jnp.mean(xc * xc, axis=-1, keepdims=True)
    return xc * lax.rsqrt(var + LN_EPS)


def _gelu(x):
    c = math.sqrt(2.0 / math.pi)
    return 0.5 * x * (1.0 + jnp.tanh(c * (x + 0.044715 * (x * x * x))))


def _sigmoid(x):
    return 1.0 / (1.0 + jnp.exp(-x))


def _softplus(x):
    return jnp.maximum(x, 0.0) + jnp.log1p(jnp.exp(-jnp.abs(x)))


def _tile(n, pref):
    if n <= pref:
        return n
    for t in range(pref - pref % LANE, 0, -LANE):
        if n % t == 0:
            return t
    raise ValueError((n, pref))


def _ada_kernel(c_ref, w_ref, b_ref, o_ref):
    c = c_ref[...]
    s = c * _sigmoid(c)
    o_ref[...] = jnp.dot(s, w_ref[...], preferred_element_type=F32,
                         precision=lax.Precision.HIGHEST) + b_ref[...]


def _ada_mods(cond, w_ada, b_ada):
    L, D, N6 = w_ada.shape
    MP = cond.shape[0]
    TN = _tile(N6, 1024)
    return pl.pallas_call(
        _ada_kernel,
        out_shape=jax.ShapeDtypeStruct((L, MP, N6), F32),
        grid=(L, N6 // TN),
        in_specs=[
            pl.BlockSpec((MP, D), lambda l, j: (0, 0)),
            pl.BlockSpec((None, D, TN), lambda l, j: (l, 0, j)),
            pl.BlockSpec((None, 1, TN), lambda l, j: (l, 0, j)),
        ],
        out_specs=pl.BlockSpec((None, MP, TN), lambda l, j: (l, 0, j)),
        compiler_params=_cp(("parallel", "parallel")),
        name="ada_mods",
    )(cond, w_ada, b_ada.reshape(L, 1, N6))


def _inproj_kernel(x_ref, mod_ref, w_ref, o_ref, h_ref):
    @pl.when(pl.program_id(1) == 0)
    def _():
        h = _ln(x_ref[...]) * (1.0 + mod_ref[1:2, :]) + mod_ref[0:1, :]
        h_ref[...] = h.astype(BF16)

    o_ref[...] = jnp.dot(h_ref[...], w_ref[...], preferred_element_type=F32)


def _inproj(xs, mods, w_in_bf, l, midx, TM):
    NT, D = xs.shape
    NC = w_in_bf.shape[-1]
    TN = _tile(NC, 1024)
    return pl.pallas_call(
        _inproj_kernel,
        out_shape=jax.ShapeDtypeStruct((NT, NC), F32),
        grid=(NT // TM, NC // TN),
        in_specs=[
            pl.BlockSpec((TM, D), lambda i, j: (i, 0)),
            pl.BlockSpec((None, None, 6, D), lambda i, j: (l, midx(i), 0, 0)),
            pl.BlockSpec((None, D, TN), lambda i, j: (l, 0, j)),
        ],
        out_specs=pl.BlockSpec((TM, TN), lambda i, j: (i, j)),
        scratch_shapes=[pltpu.VMEM((TM, D), BF16)],
        compiler_params=_cp(("parallel", "arbitrary")),
        name="inproj",
    )(xs, mods, w_in_bf)


def _rope(x, cos, sin_signed):
    lane = lax.broadcasted_iota(jnp.int32, x.shape, 1)
    half = LANE // 4
    swapped = jnp.where((lane % (2 * half)) < half,
                        pltpu.roll(x, LANE - half, 1), pltpu.roll(x, half, 1))
    return x * cos + swapped * sin_signed


def _qkprep_kernel(k_ref, v_ref, q_ref, cos_ref, sin_ref, kg_ref, qg_ref,
                   ko_ref, vo_ref, qo_ref, *, groups, scale):
    cos = cos_ref[...]
    sin = sin_ref[...]

    def norm_rope(x, g):
        ms = jnp.mean(x * x, axis=-1, keepdims=True)
        return _rope(x * lax.rsqrt(ms + LN_EPS) * g, cos, sin)

    ko_ref[...] = norm_rope(k_ref[...], kg_ref[...]).astype(BF16)
    vo_ref[...] = v_ref[...].astype(BF16)
    for h in range(groups):
        sl = slice(h * LANE, (h + 1) * LANE)
        qo_ref[:, sl] = (norm_rope(q_ref[:, sl], qg_ref[...]) * scale).astype(BF16)


def _qkprep(px, cos_t, sin_t, k_gain, q_gain, dims, l, TM):
    NT = px.shape[0]
    KVH, G, HD = dims["KVH"], dims["G"], dims["HD"]
    KV_W, CTX_COLS, ATTN_W = dims["KV_W"], dims["CTX_COLS"], dims["ATTN_W"]
    S, BS = dims["S"], dims["B"] * dims["S"]
    assert HD == LANE and CTX_COLS % (G * HD) == 0
    n_lat = BS // TM
    spb = S // TM

    def tab(i, g):
        return (jnp.where(i < n_lat, i % spb, spb), 0)

    return pl.pallas_call(
        functools.partial(_qkprep_kernel, groups=G, scale=float(HD) ** -0.5),
        out_shape=(jax.ShapeDtypeStruct((NT, KV_W), BF16),
                   jax.ShapeDtypeStruct((NT, KV_W), BF16),
                   jax.ShapeDtypeStruct((NT, ATTN_W), BF16)),
        grid=(NT // TM, KVH),
        in_specs=[
            pl.BlockSpec((TM, HD), lambda i, g: (i, g)),
            pl.BlockSpec((TM, HD), lambda i, g: (i, KVH + g)),
            pl.BlockSpec((TM, G * HD), lambda i, g: (i, CTX_COLS // (G * HD) + g)),
            pl.BlockSpec((TM, HD), tab),
            pl.BlockSpec((TM, HD), tab),
            pl.BlockSpec((None, 1, HD), lambda i, g: (l, 0, 0)),
            pl.BlockSpec((None, 1, HD), lambda i, g: (l, 0, 0)),
        ],
        out_specs=(pl.BlockSpec((TM, HD), lambda i, g: (i, g)),
                   pl.BlockSpec((TM, HD), lambda i, g: (i, g)),
                   pl.BlockSpec((TM, G * HD), lambda i, g: (i, g))),
        compiler_params=_cp(("parallel", "parallel")),
        name="qkprep",
    )(px, px, px, cos_t, sin_t, k_gain, q_gain)


def _attn_kernel(q_ref, kc_ref, vc_ref, kx_ref, vx_ref, o_ref, m_ref, l_ref, acc_ref,
                 *, groups, tq, tk, n_lat_q, n_lat_k):
    q = jnp.concatenate([q_ref[:, h * LANE:(h + 1) * LANE] for h in range(groups)], axis=0)

    def update(k, v):
        s = lax.dot_general(q, k, (((1,), (1,)), ((), ())), preferred_element_type=F32)
        m_old = m_ref[...]
        m_new = jnp.maximum(m_old, jnp.max(s, axis=-1, keepdims=True))
        p = jnp.exp(s - m_new)
        alpha = jnp.exp(m_old - m_new)
        l_ref[...] = alpha * l_ref[...] + jnp.sum(p, axis=-1, keepdims=True)
        acc_ref[...] = alpha * acc_ref[...] + jnp.dot(p.astype(BF16), v,
                                                      preferred_element_type=F32)
        m_ref[...] = m_new

    m_ref[...] = jnp.full(m_ref.shape, -jnp.inf, F32)
    l_ref[...] = jnp.zeros(l_ref.shape, F32)
    acc_ref[...] = jnp.zeros(acc_ref.shape, F32)
    update(kc_ref[...], vc_ref[...])

    n_k = jnp.where(pl.program_id(2) < n_lat_q, n_lat_k, 0)

    def body(j, carry):
        r0 = pl.multiple_of(j * tk, tk)
        update(kx_ref[pl.ds(r0, tk), :], vx_ref[pl.ds(r0, tk), :])
        return carry

    lax.fori_loop(0, n_k, body, 0)
    o = acc_ref[...] / l_ref[...]
    for h in range(groups):
        o_ref[:, h * LANE:(h + 1) * LANE] = o[h * tq:(h + 1) * tq].astype(o_ref.dtype)


def _attention(qn, kn, vb, dims, TQ):
    NT = qn.shape[0]
    B, S, C = dims["B"], dims["S"], dims["C"]
    KVH, G, HD = dims["KVH"], dims["G"], dims["HD"]
    TK = _tile(S, 512)
    nq_lat, nq_ctx = S // TQ, C // TQ

    def qrow(b, g, qi):
        return (jnp.where(qi < nq_lat, b * nq_lat + qi,
                          (B * S) // TQ + b * nq_ctx + (qi - nq_lat)), g)

    ctx_blk = lambda b, g, qi: ((B * S) // C + b, g)
    lat_blk = lambda b, g, qi: (b, g)
    return pl.pallas_call(
        functools.partial(_attn_kernel, groups=G, tq=TQ, tk=TK, n_lat_q=nq_lat, n_lat_k=S // TK),
        out_shape=jax.ShapeDtypeStruct((NT, G * KVH * HD), BF16),
        grid=(B, KVH, nq_lat + nq_ctx),
        in_specs=[
            pl.BlockSpec((TQ, G * HD), qrow),
            pl.BlockSpec((C, HD), ctx_blk),
            pl.BlockSpec((C, HD), ctx_blk),
            pl.BlockSpec((S, HD), lat_blk),
            pl.BlockSpec((S, HD), lat_blk),
        ],
        out_specs=pl.BlockSpec((TQ, G * HD), qrow),
        scratch_shapes=[pltpu.VMEM((G * TQ, 1), F32), pltpu.VMEM((G * TQ, 1), F32),
                        pltpu.VMEM((G * TQ, HD), F32)],
        compiler_params=_cp(("parallel", "parallel", "arbitrary")),
        name="attention",
    )(qn, kn, vb, kn, vb)


def _scan_tiles(a_ref, u_ref, y_ref, h0, n_tiles, reverse, accumulate):
    row = lax.broadcasted_iota(jnp.int32, (SUBLANE, LANE), 0)

    def body(i, hc):
        t = (n_tiles - 1 - i) if reverse else i
        r0 = pl.multiple_of(t * SUBLANE, SUBLANE)
        a = a_ref[pl.ds(r0, SUBLANE), :]
        u = u_ref[pl.ds(r0, SUBLANE), :]
        for sh in (1, 2, 4):
            if reverse:
                ap = pltpu.roll(a, SUBLANE - sh, 0)
                up = pltpu.roll(u, SUBLANE - sh, 0)
                m = row < SUBLANE - sh
            else:
                ap = pltpu.roll(a, sh, 0)
                up = pltpu.roll(u, sh, 0)
                m = row >= sh
            u = jnp.where(m, a * up + u, u)
            a = jnp.where(m, a * ap, a)
        h = a * hc + u
        if accumulate:
            y_ref[pl.ds(r0, SUBLANE), :] = y_ref[pl.ds(r0, SUBLANE), :] + h
        else:
            y_ref[pl.ds(r0, SUBLANE), :] = h
        return h[0:1, :] if reverse else h[SUBLANE - 1:SUBLANE, :]

    return lax.fori_loop(0, n_tiles, body, h0)


def _lru_kernel(x_ref, xg_ref, cw_ref, cb_ref,
                waf_ref, baf_ref, wxf_ref, bxf_ref, lamf_ref,
                wab_ref, bab_ref, wxb_ref, bxb_ref, lamb_ref,
                h0f_ref, h0b_ref, yin_ref,
                y_ref, hf_ref, hb_ref,
                xp_ref, xc_ref, a_ref, u_ref, acc_ref, *, T, conv_w):
    del yin_ref
    pad = SUBLANE
    xp_ref[0:pad, :] = jnp.zeros((pad, LANE), F32)
    xp_ref[pad + T:pad + T + pad, :] = jnp.zeros((pad, LANE), F32)
    xp_ref[pad:pad + T, :] = x_ref[...]
    acc = None
    for j in range(conv_w):
        term = xp_ref[pad + j - CONV_PAD_L:pad + j - CONV_PAD_L + T, :] * cw_ref[j:j + 1, :]
        acc = term if acc is None else acc + term
    xc = acc + cb_ref[...]
    xc_ref[...] = xc
    xcb = xc.astype(BF16)

    def gates(wa_ref, ba_ref, wx_ref, bx_ref, lam_ref):
        r = _sigmoid(jnp.dot(xcb, wa_ref[...], preferred_element_type=F32) + ba_ref[...])
        i = _sigmoid(jnp.dot(xcb, wx_ref[...], preferred_element_type=F32) + bx_ref[...])
        log_a = -RG_C * r * _softplus(-lam_ref[...])
        a_ref[...] = jnp.exp(log_a)
        th = jnp.tanh(log_a)
        u_ref[...] = jnp.sqrt(-2.0 * th / (1.0 - th)) * i * xc_ref[...]

    n_tiles = T // SUBLANE
    gates(waf_ref, baf_ref, wxf_ref, bxf_ref, lamf_ref)
    hf_ref[...] = _scan_tiles(a_ref, u_ref, acc_ref, h0f_ref[...], n_tiles, False, False)
    gates(wab_ref, bab_ref, wxb_ref, bxb_ref, lamb_ref)
    hb_ref[...] = _scan_tiles(a_ref, u_ref, acc_ref, h0b_ref[...], n_tiles, True, True)
    y_ref[...] = (acc_ref[...] * _gelu(xg_ref[...])).astype(y_ref.dtype)


def _lru(px, y_prev, h0f, h0b, lp, dims, l, T, row_blk0):
    NT = px.shape[0]
    B, W, KV_W, Q_END = dims["B"], dims["W"], dims["KV_W"], dims["Q_END"]
    NB = W // LANE
    col_x = (2 * KV_W) // LANE
    col_g = Q_END // LANE
    vec = lambda: pl.BlockSpec((None, 1, LANE), lambda b, n: (l, 0, n))
    mat = lambda: pl.BlockSpec((None, None, LANE, LANE), lambda b, n: (l, n, 0, 0))
    st = lambda: pl.BlockSpec((None, 1, LANE), lambda b, n: (b, 0, n))
    conv_w = lp["conv_w"].shape[1]
    y, hf, hb = pl.pallas_call(
        functools.partial(_lru_kernel, T=T, conv_w=conv_w),
        out_shape=(jax.ShapeDtypeStruct((NT, W), BF16),
                   jax.ShapeDtypeStruct((B, 1, W), F32),
                   jax.ShapeDtypeStruct((B, 1, W), F32)),
        grid=(B, NB),
        in_specs=[
            pl.BlockSpec((T, LANE), lambda b, n: (row_blk0 + b, col_x + n)),
            pl.BlockSpec((T, LANE), lambda b, n: (row_blk0 + b, col_g + n)),
            pl.BlockSpec((None, conv_w, LANE), lambda b, n: (l, 0, n)),
            vec(),
            mat(), vec(), mat(), vec(), vec(),
            mat(), vec(), mat(), vec(), vec(),
            st(), st(),
            pl.BlockSpec(memory_space=pl.ANY),
        ],
        out_specs=(pl.BlockSpec((T, LANE), lambda b, n: (row_blk0 + b, n)), st(), st()),
        scratch_shapes=[pltpu.VMEM((T + 2 * SUBLANE, LANE), F32), pltpu.VMEM((T, LANE), F32),
                        pltpu.VMEM((T, LANE), F32), pltpu.VMEM((T, LANE), F32),
                        pltpu.VMEM((T, LANE), F32)],
        input_output_aliases={16: 0},
        compiler_params=_cp(("parallel", "parallel")),
        name="rglru",
    )(px, px, lp["conv_w"], lp["conv_b"],
      lp["wa_f"], lp["ba_f"], lp["wx_f"], lp["bx_f"], lp["lam_f"],
      lp["wa_b"], lp["ba_b"], lp["wx_b"], lp["bx_b"], lp["lam_b"],
      h0f, h0b, y_prev)
    return y, hf, hb


def _merge_kernel(at_ref, y_ref, wa_ref, wl_ref, ga_ref, gl_ref, o_ref):
    a = jnp.dot(at_ref[...], wa_ref[...], preferred_element_type=F32)
    b = jnp.dot(y_ref[...], wl_ref[...], preferred_element_type=F32)
    o_ref[...] = (_sigmoid(ga_ref[...]) * a + _sigmoid(gl_ref[...]) * b).astype(o_ref.dtype)


def _merge(attn_o, y, px, wa_bf, wl_bf, dims, l, TM):
    NT = attn_o.shape[0]
    D, XG_END = dims["D"], dims["XG_END"]
    KA, KL = wa_bf.shape[1], wl_bf.shape[1]
    TN = _tile(D, 1024)
    assert XG_END % TN == 0
    return pl.pallas_call(
        _merge_kernel,
        out_shape=jax.ShapeDtypeStruct((NT, D), BF16),
        grid=(NT // TM, D // TN),
        in_specs=[
            pl.BlockSpec((TM, KA), lambda i, j: (i, 0)),
            pl.BlockSpec((TM, KL), lambda i, j: (i, 0)),
            pl.BlockSpec((None, KA, TN), lambda i, j: (l, 0, j)),
            pl.BlockSpec((None, KL, TN), lambda i, j: (l, 0, j)),
            pl.BlockSpec((TM, TN), lambda i, j: (i, XG_END // TN + j)),
            pl.BlockSpec((TM, TN), lambda i, j: (i, (XG_END + D) // TN + j)),
        ],
        out_specs=pl.BlockSpec((TM, TN), lambda i, j: (i, j)),
        compiler_params=_cp(("parallel", "arbitrary")),
        name="merge",
    )(attn_o, y, wa_bf, wl_bf, px, px)


def _outproj_kernel(mx_ref, w_ref, x_ref, mod_ref, g_ref, b_ref, x1_ref, h2b_ref, h2r_ref,
                    *, alpha):
    mix = jnp.dot(mx_ref[...], w_ref[...], preferred_element_type=F32)
    x1 = _ln(alpha * x_ref[...] + mod_ref[2:3, :] * mix) * g_ref[...] + b_ref[...]
    x1_ref[...] = x1
    h2 = (_ln(x1) * (1.0 + mod_ref[4:5, :]) + mod_ref[3:4, :]).astype(BF16)
    h2b_ref[...] = h2
    h2r_ref[...] = h2.astype(F32)


def _outproj(mixp, w_out_bf, xs, mods, ln_g, ln_b, l, midx, alpha, TM):
    NT, D = xs.shape
    row = lambda: pl.BlockSpec((TM, D), lambda i: (i, 0))
    return pl.pallas_call(
        functools.partial(_outproj_kernel, alpha=alpha),
        out_shape=(jax.ShapeDtypeStruct((NT, D), F32),
                   jax.ShapeDtypeStruct((NT, D), BF16),
                   jax.ShapeDtypeStruct((NT, D), F32)),
        grid=(NT // TM,),
        in_specs=[
            row(),
            pl.BlockSpec((None, D, D), lambda i: (l, 0, 0)),
            row(),
            pl.BlockSpec((None, None, 6, D), lambda i: (l, midx(i), 0, 0)),
            pl.BlockSpec((None, 1, D), lambda i: (l, 0, 0)),
            pl.BlockSpec((None, 1, D), lambda i: (l, 0, 0)),
        ],
        out_specs=(row(), row(), row()),
        compiler_params=_cp(("parallel",)),
        name="outproj_norm1",
    )(mixp, w_out_bf, xs, mods, ln_g, ln_b)


def _extract_topk(s, ids, k, on_pick):
    n = s.shape[-1]
    lane = lax.broadcasted_iota(jnp.int32, s.shape, 1)
    for r in range(k):
        m = jnp.max(s, axis=-1, keepdims=True)
        pos = jnp.min(jnp.where(s == m, lane, n), axis=-1, keepdims=True)
        sel = lane == pos
        e = pos if ids is None else jnp.sum(jnp.where(sel, ids, 0), axis=-1, keepdims=True)
        on_pick(r, m, e)
        s = jnp.where(sel, -jnp.inf, s)


def _route_kernel(h_ref, wq_ref, k1_ref, k2_ref, idx_ref, gate_ref, *, heads, dk, n_keys, topk):
    tm = h_ref.shape[0]
    q = jnp.dot(h_ref[...], wq_ref[...], preferred_element_type=F32).astype(BF16)
    nsel = heads * topk
    lane_o = lax.broadcasted_iota(jnp.int32, (tm, nsel), 1)
    lane_c = lax.broadcasted_iota(jnp.int32, (tm, topk * topk), 1)
    out_i = jnp.zeros((tm, nsel), jnp.int32)
    out_v = jnp.zeros((tm, nsel), F32)
    out_d = jnp.zeros((tm, nsel), F32)
    dn = (((1,), (1,)), ((), ()))
    for h in range(heads):
        q1 = q[:, (2 * h) * dk:(2 * h + 1) * dk]
        q2 = q[:, (2 * h + 1) * dk:(2 * h + 2) * dk]
        s1 = lax.dot_general(q1, k1_ref[h], dn, preferred_element_type=F32)
        s2 = lax.dot_general(q2, k2_ref[h], dn, preferred_element_type=F32)
        cand = {"v": jnp.zeros((tm, topk * topk), F32),
                "i": jnp.zeros((tm, topk * topk), jnp.int32)}

        def pick1(r, m, e, cand=cand):
            hit = (lane_c // topk) == r
            cand["v"] = cand["v"] + jnp.where(hit, m, 0.0)
            cand["i"] = cand["i"] + jnp.where(hit, e * n_keys, 0)

        def pick2(r, m, e, cand=cand):
            hit = (lane_c % topk) == r
            cand["v"] = cand["v"] + jnp.where(hit, m, 0.0)
            cand["i"] = cand["i"] + jnp.where(hit, e, 0)

        _extract_topk(s1, None, topk, pick1)
        _extract_topk(s2, None, topk, pick2)
        fin = {"i": out_i, "v": out_v, "d": jnp.zeros((tm, 1), F32), "m0": None}

        def pick3(r, m, e, fin=fin, h=h):
            if r == 0:
                fin["m0"] = m
            hit = lane_o == (h * topk + r)
            fin["i"] = jnp.where(hit, e, fin["i"])
            fin["v"] = jnp.where(hit, m - fin["m0"], fin["v"])
            fin["d"] = fin["d"] + jnp.exp(m - fin["m0"])

        _extract_topk(cand["v"], cand["i"], topk, pick3)
        out_i, out_v = fin["i"], fin["v"]
        out_d = jnp.where((lane_o // topk) == h, fin["d"], out_d)
    idx_ref[...] = out_i
    gate_ref[...] = jnp.exp(out_v) / out_d


def _route(h2b, w_pq_bf, k1_bf, k2_bf, l, TM):
    NT, D = h2b.shape
    H, NK, DK = k1_bf.shape[1:]
    PQ = w_pq_bf.shape[-1]
    nsel = H * PEER_TOPK
    return pl.pallas_call(
        functools.partial(_route_kernel, heads=H, dk=DK, n_keys=NK, topk=PEER_TOPK),
        out_shape=(jax.ShapeDtypeStruct((NT, nsel), jnp.int32),
                   jax.ShapeDtypeStruct((NT, nsel), F32)),
        grid=(NT // TM,),
        in_specs=[
            pl.BlockSpec((TM, D), lambda i: (i, 0)),
            pl.BlockSpec((None, D, PQ), lambda i: (l, 0, 0)),
            pl.BlockSpec((None, H, NK, DK), lambda i: (l, 0, 0, 0)),
            pl.BlockSpec((None, H, NK, DK), lambda i: (l, 0, 0, 0)),
        ],
        out_specs=(pl.BlockSpec((TM, nsel), lambda i: (i, 0)),
                   pl.BlockSpec((TM, nsel), lambda i: (i, 0))),
        compiler_params=_cp(("parallel",)),
        name="peer_route",
    )(h2b, w_pq_bf, k1_bf, k2_bf)


def _peer_kernel(idx_ref, h_ref, gate_ref, x1_ref, mod_ref, g_ref, b_ref, tab_ref,
                 o_ref, buf_ref, ffn_ref, sem, *, tb, nsel, dj, nbuf, alpha):
    def issue(t, slot):
        for k in range(nsel):
            e = idx_ref[t, k]
            pltpu.make_async_copy(tab_ref.at[e], buf_ref.at[slot, k], sem.at[slot]).start()

    def wait(slot):
        pltpu.make_async_copy(buf_ref.at[slot], buf_ref.at[slot], sem.at[slot]).wait()

    for t in range(nbuf - 1):
        issue(t, t)

    eye = (lax.broadcasted_iota(jnp.int32, (nsel, nsel), 0)
           == lax.broadcasted_iota(jnp.int32, (nsel, nsel), 1))
    hi_mask = jnp.uint32(0xFFFF0000)

    def body(t, carry):
        slot = t % nbuf

        @pl.when(t + nbuf - 1 < tb)
        def _():
            issue(t + nbuf - 1, (t + nbuf - 1) % nbuf)

        wait(slot)
        hrow = h_ref[pl.ds(t, 1), :]
        acc = jnp.zeros((nsel, LANE), F32)
        for j in range(dj):
            w = buf_ref[slot, :, j, :]
            u = lax.bitcast_convert_type(w << 16, F32)
            acc = acc + u * hrow[:, j * LANE:(j + 1) * LANE]
        s = jnp.sum(acc, axis=-1, keepdims=True)
        gcol = jnp.sum(jnp.where(eye, gate_ref[pl.ds(t, 1), :], 0.0), axis=-1, keepdims=True)
        act = _gelu(s) * gcol
        t8 = pl.multiple_of((t // SUBLANE) * SUBLANE, SUBLANE)
        mine = row8 == (t % SUBLANE)
        for j in range(dj):
            w = buf_ref[slot, :, j, :]
            v = lax.bitcast_convert_type(w & hi_mask, F32)
            r = jnp.sum(v * act, axis=0, keepdims=True)
            sl = slice(j * LANE, (j + 1) * LANE)
            ffn_ref[pl.ds(t8, SUBLANE), sl] = jnp.where(mine, r, ffn_ref[pl.ds(t8, SUBLANE), sl])
        return carry

    row8 = lax.broadcasted_iota(jnp.int32, (SUBLANE, LANE), 0)
    ffn_ref[...] = jnp.zeros(ffn_ref.shape, F32)
    lax.fori_loop(0, tb, body, 0)
    x2 = _ln(alpha * x1_ref[...] + mod_ref[5:6, :] * ffn_ref[...]) * g_ref[...] + b_ref[...]
    o_ref[...] = x2


def _peer(idx, h2r, gates, x1, mods, ln_g, ln_b, table, l, midx, alpha, TB):
    NT, D = x1.shape
    nsel = idx.shape[1]
    DJ = D // LANE
    nbuf = 8
    row = lambda: pl.BlockSpec((TB, D), lambda i: (i, 0))
    return pl.pallas_call(
        functools.partial(_peer_kernel, tb=TB, nsel=nsel, dj=DJ, nbuf=nbuf, alpha=alpha),
        out_shape=jax.ShapeDtypeStruct((NT, D), F32),
        grid=(NT // TB,),
        in_specs=[
            pl.BlockSpec((TB, nsel), lambda i: (i, 0), memory_space=pltpu.SMEM),
            row(),
            pl.BlockSpec((TB, nsel), lambda i: (i, 0)),
            row(),
            pl.BlockSpec((None, None, 6, D), lambda i: (l, midx(i), 0, 0)),
            pl.BlockSpec((None, 1, D), lambda i: (l, 0, 0)),
            pl.BlockSpec((None, 1, D), lambda i: (l, 0, 0)),
            pl.BlockSpec(memory_space=pl.ANY),
        ],
        out_specs=row(),
        scratch_shapes=[pltpu.VMEM((nbuf, nsel, DJ, LANE), jnp.uint32),
                        pltpu.VMEM((TB, D), F32),
                        pltpu.SemaphoreType.DMA((nbuf,))],
        compiler_params=_cp(("arbitrary",)),
        name="peer_experts",
    )(idx, h2r, gates, x1, mods, ln_g, ln_b, table)


def _pack_table(u, v):
    E, D = u.shape
    ub = lax.bitcast_convert_type(u.astype(BF16), jnp.uint16).astype(jnp.uint32)
    vb = lax.bitcast_convert_type(v.astype(BF16), jnp.uint16).astype(jnp.uint32)
    return (ub | (vb << 16)).reshape(E, D // LANE, LANE)


def _rope_tables(S, TM):
    rows = S // GRID_W
    axis_dim = LANE // 2
    row = jnp.repeat(jnp.arange(rows, dtype=F32), GRID_W)
    col = jnp.tile(jnp.arange(GRID_W, dtype=F32), rows)
    inv_freq = ROPE_THETA ** (-jnp.arange(0, axis_dim, 2, dtype=F32) / axis_dim)
    ar = row[:, None] * inv_freq
    ac = col[:, None] * inv_freq
    cos = jnp.concatenate([jnp.cos(ar), jnp.cos(ar), jnp.cos(ac), jnp.cos(ac)], axis=1)
    sin = jnp.concatenate([-jnp.sin(ar), jnp.sin(ar), -jnp.sin(ac), jnp.sin(ac)], axis=1)
    cos = jnp.concatenate([cos, jnp.ones((TM, LANE), F32)], axis=0)
    sin = jnp.concatenate([sin, jnp.zeros((TM, LANE), F32)], axis=0)
    return cos, sin


def kernel(x, c, ctx, c_ctx, w_ada, b_ada, w_in, q_gain, k_gain, conv_w, conv_b, wa_f, ba_f, wx_f, bx_f, lam_f, wa_b, ba_b, wx_b, bx_b, lam_b, w_attn_br, w_lru_br, w_out, ln1_g, ln1_b, w_pq, sub_k1, sub_k2, peer_u, peer_v, ln2_g, ln2_b):
    B, S, D = x.shape
    C = ctx.shape[1]
    L = w_in.shape[0]
    HD = q_gain.shape[-1]
    W = conv_w.shape[-1]
    ATTN_W = w_attn_br.shape[1]
    IN_COLS = w_in.shape[-1]
    KV_W = (IN_COLS - 2 * W - ATTN_W - 2 * D) // 2
    KVH = KV_W // HD
    dims = dict(B=B, S=S, C=C, D=D, HD=HD, W=W, ATTN_W=ATTN_W, KV_W=KV_W, KVH=KVH,
                G=(ATTN_W // HD) // KVH, CTX_COLS=2 * KV_W + W,
                Q_END=2 * KV_W + W + ATTN_W, XG_END=2 * KV_W + 2 * W + ATTN_W)
    alpha = float((2 * L) ** 0.25)
    NT = B * S + B * C

    TM = _tile(C, 256)
    TBIG = _tile(math.gcd(S, B * C), 512)
    TB = _tile(C, 128)

    def midx_for(tm):
        n_lat, per_b = (B * S) // tm, S // tm
        return lambda i: jnp.where(i < n_lat, i // per_b, B)

    MP = -(-(B + 1) // SUBLANE) * SUBLANE
    cond = jnp.zeros((MP, D), F32).at[:B].set(c).at[B].set(c_ctx)
    mods = _ada_mods(cond, w_ada, b_ada).reshape(L, MP, 6, D)

    xs = jnp.concatenate([x.reshape(B * S, D), ctx.reshape(B * C, D)], axis=0)
    cos_t, sin_t = _rope_tables(S, TM)

    w_in_bf = w_in.astype(BF16)
    wa_bf, wl_bf, wo_bf, wq_bf = (w.astype(BF16) for w in (w_attn_br, w_lru_br, w_out, w_pq))
    k1_bf, k2_bf = sub_k1.astype(BF16), sub_k2.astype(BF16)
    vec3 = lambda a: a.reshape(L, 1, a.shape[-1])
    lp = dict(conv_w=conv_w, conv_b=vec3(conv_b),
              wa_f=wa_f.astype(BF16), ba_f=vec3(ba_f.reshape(L, W)),
              wx_f=wx_f.astype(BF16), bx_f=vec3(bx_f.reshape(L, W)), lam_f=vec3(lam_f),
              wa_b=wa_b.astype(BF16), ba_b=vec3(ba_b.reshape(L, W)),
              wx_b=wx_b.astype(BF16), bx_b=vec3(bx_b.reshape(L, W)), lam_b=vec3(lam_b))
    qg, kg = vec3(q_gain), vec3(k_gain)
    g1, b1, g2, b2 = vec3(ln1_g), vec3(ln1_b), vec3(ln2_g), vec3(ln2_b)
    zeros_state = jnp.zeros((B, 1, W), F32)
    y = jnp.zeros((NT, W), BF16)

    for l in range(L):
        px = _inproj(xs, mods, w_in_bf, l, midx_for(TBIG), TBIG)
        kn, vb, qn = _qkprep(px, cos_t, sin_t, kg, qg, dims, l, TM)
        attn_o = _attention(qn, kn, vb, dims, TM)
        y, hcf, hcb = _lru(px, y, zeros_state, zeros_state, lp, dims, l, C, (B * S) // C)
        y, _, _ = _lru(px, y, hcf, hcb, lp, dims, l, S, 0)
        mixp = _merge(attn_o, y, px, wa_bf, wl_bf, dims, l, TBIG)
        x1, h2b, h2r = _outproj(mixp, wo_bf, xs, mods, g1, b1, l, midx_for(TM), alpha, TM)
        idx, gates = _route(h2b, wq_bf, k1_bf, k2_bf, l, TM)
        table = _pack_table(peer_u[l], peer_v[l])
        xs = _peer(idx, h2r, gates, x1, mods, g2, b2, table, l, midx_for(TB), alpha, TB)
    return xs[:B * S].reshape(B, S, D)
```

```python
import functools
import math

import jax
import jax.numpy as jnp
from jax import lax
from jax.experimental import pallas as pl
from jax.experimental.pallas import tpu as pltpu

GRID_W = 64
ROPE_THETA = 10000.0
RG_C = 8.0
CONV_PAD_L = 2
PEER_TOPK = 16
LN_EPS = 1e-6

LANE = 128
SUBLANE = 8
VMEM_LIMIT = 56 * 1024 * 1024

F32 = jnp.float32
BF16 = jnp.bfloat16


def _cp(sem, vmem=VMEM_LIMIT):
    return pltpu.CompilerParams(dimension_semantics=sem, vmem_limit_bytes=vmem)


def _ln(x):
    mu = jnp.mean(x, axis=-1, keepdims=True)
    xc = x - mu
    var = jnp.mean(xc * xc, axis=-1, keepdims=True)
    return xc * lax.rsqrt(var + LN_EPS)


def _gelu(x):
    c = math.sqrt(2.0 / math.pi)
    return 0.5 * x * (1.0 + jnp.tanh(c * (x + 0.044715 * (x * x * x))))


def _sigmoid(x):
    return 1.0 / (1.0 + jnp.exp(-x))


def _softplus(x):
    return jnp.maximum(x, 0.0) + jnp.log1p(jnp.exp(-jnp.abs(x)))


def _tile(n, pref):
    if n <= pref:
        return n
    for t in range(pref - pref % LANE, 0, -LANE):
        if n % t == 0:
            return t
    raise ValueError((n, pref))


def _ada_kernel(c_ref, w_ref, b_ref, o_ref):
    c = c_ref[...]
    s = c * _sigmoid(c)
    o_ref[...] = jnp.dot(s, w_ref[...], preferred_element_type=F32,
                         precision=lax.Precision.HIGHEST) + b_ref[...]


def _ada_mods(cond, w_ada, b_ada):
    L, D, N6 = w_ada.shape
    MP = cond.shape[0]
    TN = _tile(N6, 1024)
    return pl.pallas_call(
        _ada_kernel,
        out_shape=jax.ShapeDtypeStruct((L, MP, N6), F32),
        grid=(L, N6 // TN),
        in_specs=[
            pl.BlockSpec((MP, D), lambda l, j: (0, 0)),
            pl.BlockSpec((None, D, TN), lambda l, j: (l, 0, j)),
            pl.BlockSpec((None, 1, TN), lambda l, j: (l, 0, j)),
        ],
        out_specs=pl.BlockSpec((None, MP, TN), lambda l, j: (l, 0, j)),
        compiler_params=_cp(("parallel", "parallel")),
        name="ada_mods",
    )(cond, w_ada, b_ada.reshape(L, 1, N6))


def _inproj_kernel(x_ref, mod_ref, w_ref, o_ref, h_ref):
    @pl.when(pl.program_id(1) == 0)
    def _():
        h = _ln(x_ref[...]) * (1.0 + mod_ref[1:2, :]) + mod_ref[0:1, :]
        h_ref[...] = h.astype(BF16)

    o_ref[...] = jnp.dot(h_ref[...], w_ref[...], preferred_element_type=F32)


def _inproj(xs, mods, w_in_bf, l, midx, TM):
    NT, D = xs.shape
    NC = w_in_bf.shape[-1]
    TN = _tile(NC, 1024)
    return pl.pallas_call(
        _inproj_kernel,
        out_shape=jax.ShapeDtypeStruct((NT, NC), F32),
        grid=(NT // TM, NC // TN),
        in_specs=[
            pl.BlockSpec((TM, D), lambda i, j: (i, 0)),
            pl.BlockSpec((None, None, 6, D), lambda i, j: (l, midx(i), 0, 0)),
            pl.BlockSpec((None, D, TN), lambda i, j: (l, 0, j)),
        ],
        out_specs=pl.BlockSpec((TM, TN), lambda i, j: (i, j)),
        scratch_shapes=[pltpu.VMEM((TM, D), BF16)],
        compiler_params=_cp(("parallel", "arbitrary")),
        name="inproj",
    )(xs, mods, w_in_bf)


def _rope(x, cos, sin_signed):
    lane = lax.broadcasted_iota(jnp.int32, x.shape, 1)
    half = LANE // 4
    swapped = jnp.where((lane % (2 * half)) < half,
                        pltpu.roll(x, LANE - half, 1), pltpu.roll(x, half, 1))
    return x * cos + swapped * sin_signed


def _qkprep_kernel(k_ref, v_ref, q_ref, cos_ref, sin_ref, kg_ref, qg_ref,
                   ko_ref, vo_ref, qo_ref, *, groups, scale):
    cos = cos_ref[...]
    sin = sin_ref[...]

    def norm_rope(x, g):
        ms = jnp.mean(x * x, axis=-1, keepdims=True)
        return _rope(x * lax.rsqrt(ms + LN_EPS) * g, cos, sin)

    ko_ref[...] = norm_rope(k_ref[...], kg_ref[...]).astype(BF16)
    vo_ref[...] = v_ref[...].astype(BF16)
    for h in range(groups):
        sl = slice(h * LANE, (h + 1) * LANE)
        qo_ref[:, sl] = (norm_rope(q_ref[:, sl], qg_ref[...]) * scale).astype(BF16)


def _qkprep(px, cos_t, sin_t, k_gain, q_gain, dims, l, TM):
    NT = px.shape[0]
    KVH, G, HD = dims["KVH"], dims["G"], dims["HD"]
    KV_W, CTX_COLS, ATTN_W = dims["KV_W"], dims["CTX_COLS"], dims["ATTN_W"]
    S, BS = dims["S"], dims["B"] * dims["S"]
    assert HD == LANE and CTX_COLS % (G * HD) == 0
    n_lat = BS // TM
    spb = S // TM

    def tab(i, g):
        return (jnp.where(i < n_lat, i % spb, spb), 0)

    return pl.pallas_call(
        functools.partial(_qkprep_kernel, groups=G, scale=float(HD) ** -0.5),
        out_shape=(jax.ShapeDtypeStruct((NT, KV_W), BF16),
                   jax.ShapeDtypeStruct((NT, KV_W), BF16),
                   jax.ShapeDtypeStruct((NT, ATTN_W), BF16)),
        grid=(NT // TM, KVH),
        in_specs=[
            pl.BlockSpec((TM, HD), lambda i, g: (i, g)),
            pl.BlockSpec((TM, HD), lambda i, g: (i, KVH + g)),
            pl.BlockSpec((TM, G * HD), lambda i, g: (i, CTX_COLS // (G * HD) + g)),
            pl.BlockSpec((TM, HD), tab),
            pl.BlockSpec((TM, HD), tab),
            pl.BlockSpec((None, 1, HD), lambda i, g: (l, 0, 0)),
            pl.BlockSpec((None, 1, HD), lambda i, g: (l, 0, 0)),
        ],
        out_specs=(pl.BlockSpec((TM, HD), lambda i, g: (i, g)),
                   pl.BlockSpec((TM, HD), lambda i, g: (i, g)),
                   pl.BlockSpec((TM, G * HD), lambda i, g: (i, g))),
        compiler_params=_cp(("parallel", "parallel")),
        name="qkprep",
    )(px, px, px, cos_t, sin_t, k_gain, q_gain)


def _attn_kernel(q_ref, kc_ref, vc_ref, kx_ref, vx_ref, o_ref, m_ref, l_ref, acc_ref,
                 *, groups, tq, tk, n_lat_q, n_lat_k):
    q = jnp.concatenate([q_ref[:, h * LANE:(h + 1) * LANE] for h in range(groups)], axis=0)

    def update(k, v):
        s = lax.dot_general(q, k, (((1,), (1,)), ((), ())), preferred_element_type=F32)
        m_old = m_ref[...]
        m_new = jnp.maximum(m_old, jnp.max(s, axis=-1, keepdims=True))
        p = jnp.exp(s - m_new)
        alpha = jnp.exp(m_old - m_new)
        l_ref[...] = alpha * l_ref[...] + jnp.sum(p, axis=-1, keepdims=True)
        acc_ref[...] = alpha * acc_ref[...] + jnp.dot(p.astype(BF16), v,
                                                      preferred_element_type=F32)
        m_ref[...] = m_new

    m_ref[...] = jnp.full(m_ref.shape, -jnp.inf, F32)
    l_ref[...] = jnp.zeros(l_ref.shape, F32)
    acc_ref[...] = jnp.zeros(acc_ref.shape, F32)
    update(kc_ref[...], vc_ref[...])

    n_k = jnp.where(pl.program_id(2) < n_lat_q, n_lat_k, 0)

    def body(j, carry):
        r0 = pl.multiple_of(j * tk, tk)
        update(kx_ref[pl.ds(r0, tk), :], vx_ref[pl.ds(r0, tk), :])
        return carry

    lax.fori_loop(0, n_k, body, 0)
    o = acc_ref[...] / l_ref[...]
    for h in range(groups):
        o_ref[:, h * LANE:(h + 1) * LANE] = o[h * tq:(h + 1) * tq].astype(o_ref.dtype)


def _attention(qn, kn, vb, dims, TQ):
    NT = qn.shape[0]
    B, S, C = dims["B"], dims["S"], dims["C"]
    KVH, G, HD = dims["KVH"], dims["G"], dims["HD"]
    TK = _tile(S, 512)
    nq_lat, nq_ctx = S // TQ, C // TQ

    def qrow(b, g, qi):
        return (jnp.where(qi < nq_lat, b * nq_lat + qi,
                          (B * S) // TQ + b * nq_ctx + (qi - nq_lat)), g)

    ctx_blk = lambda b, g, qi: ((B * S) // C + b, g)
    lat_blk = lambda b, g, qi: (b, g)
    return pl.pallas_call(
        functools.partial(_attn_kernel, groups=G, tq=TQ, tk=TK, n_lat_q=nq_lat, n_lat_k=S // TK),
        out_shape=jax.ShapeDtypeStruct((NT, G * KVH * HD), BF16),
        grid=(B, KVH, nq_lat + nq_ctx),
        in_specs=[
            pl.BlockSpec((TQ, G * HD), qrow),
            pl.BlockSpec((C, HD), ctx_blk),
            pl.BlockSpec((C, HD), ctx_blk),
            pl.BlockSpec((S, HD), lat_blk),
            pl.BlockSpec((S, HD), lat_blk),
        ],
        out_specs=pl.BlockSpec((TQ, G * HD), qrow),
        scratch_shapes=[pltpu.VMEM((G * TQ, 1), F32), pltpu.VMEM((G * TQ, 1), F32),
                        pltpu.VMEM((G * TQ, HD), F32)],
        compiler_params=_cp(("parallel", "parallel", "arbitrary")),
        name="attention",
    )(qn, kn, vb, kn, vb)


def _scan_tiles(a_ref, u_ref, y_ref, h0, n_tiles, reverse, accumulate):
    row = lax.broadcasted_iota(jnp.int32, (SUBLANE, LANE), 0)

    def body(i, hc):
        t = (n_tiles - 1 - i) if reverse else i
        r0 = pl.multiple_of(t * SUBLANE, SUBLANE)
        a = a_ref[pl.ds(r0, SUBLANE), :]
        u = u_ref[pl.ds(r0, SUBLANE), :]
        for sh in (1, 2, 4):
            if reverse:
                ap = pltpu.roll(a, SUBLANE - sh, 0)
                up = pltpu.roll(u, SUBLANE - sh, 0)
                m = row < SUBLANE - sh
            else:
                ap = pltpu.roll(a, sh, 0)
                up = pltpu.roll(u, sh, 0)
                m = row >= sh
            u = jnp.where(m, a * up + u, u)
            a = jnp.where(m, a * ap, a)
        h = a * hc + u
        if accumulate:
            y_ref[pl.ds(r0, SUBLANE), :] = y_ref[pl.ds(r0, SUBLANE), :] + h
        else:
            y_ref[pl.ds(r0, SUBLANE), :] = h
        return h[0:1, :] if reverse else h[SUBLANE - 1:SUBLANE, :]

    return lax.fori_loop(0, n_tiles, body, h0)


def _lru_kernel(x_ref, xg_ref, cw_ref, cb_ref,
                waf_ref, baf_ref, wxf_ref, bxf_ref, lamf_ref,
                wab_ref, bab_ref, wxb_ref, bxb_ref, lamb_ref,
                h0f_ref, h0b_ref, yin_ref,
                y_ref, hf_ref, hb_ref,
                xp_ref, xc_ref, a_ref, u_ref, acc_ref, *, T, conv_w):
    del yin_ref
    pad = SUBLANE
    xp_ref[0:pad, :] = jnp.zeros((pad, LANE), F32)
    xp_ref[pad + T:pad + T + pad, :] = jnp.zeros((pad, LANE), F32)
    xp_ref[pad:pad + T, :] = x_ref[...]
    acc = None
    for j in range(conv_w):
        term = xp_ref[pad + j - CONV_PAD_L:pad + j - CONV_PAD_L + T, :] * cw_ref[j:j + 1, :]
        acc = term if acc is None else acc + term
    xc = acc + cb_ref[...]
    xc_ref[...] = xc
    xcb = xc.astype(BF16)

    def gates(wa_ref, ba_ref, wx_ref, bx_ref, lam_ref):
        r = _sigmoid(jnp.dot(xcb, wa_ref[...], preferred_element_type=F32) + ba_ref[...])
        i = _sigmoid(jnp.dot(xcb, wx_ref[...], preferred_element_type=F32) + bx_ref[...])
        log_a = -RG_C * r * _softplus(-lam_ref[...])
        a_ref[...] = jnp.exp(log_a)
        th = jnp.tanh(log_a)
        u_ref[...] = jnp.sqrt(-2.0 * th / (1.0 - th)) * i * xc_ref[...]

    n_tiles = T // SUBLANE
    gates(waf_ref, baf_ref, wxf_ref, bxf_ref, lamf_ref)
    hf_ref[...] = _scan_tiles(a_ref, u_ref, acc_ref, h0f_ref[...], n_tiles, False, False)
    gates(wab_ref, bab_ref, wxb_ref, bxb_ref, lamb_ref)
    hb_ref[...] = _scan_tiles(a_ref, u_ref, acc_ref, h0b_ref[...], n_tiles, True, True)
    y_ref[...] = (acc_ref[...] * _gelu(xg_ref[...])).astype(y_ref.dtype)


def _lru(px, y_prev, h0f, h0b, lp, dims, l, T, row_blk0):
    NT = px.shape[0]
    B, W, KV_W, Q_END = dims["B"], dims["W"], dims["KV_W"], dims["Q_END"]
    NB = W // LANE
    col_x = (2 * KV_W) // LANE
    col_g = Q_END // LANE
    vec = lambda: pl.BlockSpec((None, 1, LANE), lambda b, n: (l, 0, n))
    mat = lambda: pl.BlockSpec((None, None, LANE, LANE), lambda b, n: (l, n, 0, 0))
    st = lambda: pl.BlockSpec((None, 1, LANE), lambda b, n: (b, 0, n))
    conv_w = lp["conv_w"].shape[1]
    y, hf, hb = pl.pallas_call(
        functools.partial(_lru_kernel, T=T, conv_w=conv_w),
        out_shape=(jax.ShapeDtypeStruct((NT, W), BF16),
                   jax.ShapeDtypeStruct((B, 1, W), F32),
                   jax.ShapeDtypeStruct((B, 1, W), F32)),
        grid=(B, NB),
        in_specs=[
            pl.BlockSpec((T, LANE), lambda b, n: (row_blk0 + b, col_x + n)),
            pl.BlockSpec((T, LANE), lambda b, n: (row_blk0 + b, col_g + n)),
            pl.BlockSpec((None, conv_w, LANE), lambda b, n: (l, 0, n)),
            vec(),
            mat(), vec(), mat(), vec(), vec(),
            mat(), vec(), mat(), vec(), vec(),
            st(), st(),
            pl.BlockSpec(memory_space=pl.ANY),
        ],
        out_specs=(pl.BlockSpec((T, LANE), lambda b, n: (row_blk0 + b, n)), st(), st()),
        scratch_shapes=[pltpu.VMEM((T + 2 * SUBLANE, LANE), F32), pltpu.VMEM((T, LANE), F32),
                        pltpu.VMEM((T, LANE), F32), pltpu.VMEM((T, LANE), F32),
                        pltpu.VMEM((T, LANE), F32)],
        input_output_aliases={16: 0},
        compiler_params=_cp(("parallel", "parallel")),
        name="rglru",
    )(px, px, lp["conv_w"], lp["conv_b"],
      lp["wa_f"], lp["ba_f"], lp["wx_f"], lp["bx_f"], lp["lam_f"],
      lp["wa_b"], lp["ba_b"], lp["wx_b"], lp["bx_b"], lp["lam_b"],
      h0f, h0b, y_prev)
    return y, hf, hb


def _merge_kernel(at_ref, y_ref, wa_ref, wl_ref, ga_ref, gl_ref, o_ref):
    a = jnp.dot(at_ref[...], wa_ref[...], preferred_element_type=F32)
    b = jnp.dot(y_ref[...], wl_ref[...], preferred_element_type=F32)
    o_ref[...] = (_sigmoid(ga_ref[...]) * a + _sigmoid(gl_ref[...]) * b).astype(o_ref.dtype)


def _merge(attn_o, y, px, wa_bf, wl_bf, dims, l, TM):
    NT = attn_o.shape[0]
    D, XG_END = dims["D"], dims["XG_END"]
    KA, KL = wa_bf.shape[1], wl_bf.shape[1]
    TN = _tile(D, 1024)
    assert XG_END % TN == 0
    return pl.pallas_call(
        _merge_kernel,
        out_shape=jax.ShapeDtypeStruct((NT, D), BF16),
        grid=(NT // TM, D // TN),
        in_specs=[
            pl.BlockSpec((TM, KA), lambda i, j: (i, 0)),
            pl.BlockSpec((TM, KL), lambda i, j: (i, 0)),
            pl.BlockSpec((None, KA, TN), lambda i, j: (l, 0, j)),
            pl.BlockSpec((None, KL, TN), lambda i, j: (l, 0, j)),
            pl.BlockSpec((TM, TN), lambda i, j: (i, XG_END // TN + j)),
            pl.BlockSpec((TM, TN), lambda i, j: (i, (XG_END + D) // TN + j)),
        ],
        out_specs=pl.BlockSpec((TM, TN), lambda i, j: (i, j)),
        compiler_params=_cp(("parallel", "arbitrary")),
        name="merge",
    )(attn_o, y, wa_bf, wl_bf, px, px)


def _outproj_kernel(mx_ref, w_ref, x_ref, mod_ref, g_ref, b_ref, x1_ref, h2b_ref, h2r_ref,
                    *, alpha):
    mix = jnp.dot(mx_ref[...], w_ref[...], preferred_element_type=F32)
    x1 = _ln(alpha * x_ref[...] + mod_ref[2:3, :] * mix) * g_ref[...] + b_ref[...]
    x1_ref[...] = x1
    h2 = (_ln(x1) * (1.0 + mod_ref[4:5, :]) + mod_ref[3:4, :]).astype(BF16)
    h2b_ref[...] = h2
    h2r_ref[...] = h2.astype(F32)


def _outproj(mixp, w_out_bf, xs, mods, ln_g, ln_b, l, midx, alpha, TM):
    NT, D = xs.shape
    row = lambda: pl.BlockSpec((TM, D), lambda i: (i, 0))
    return pl.pallas_call(
        functools.partial(_outproj_kernel, alpha=alpha),
        out_shape=(jax.ShapeDtypeStruct((NT, D), F32),
                   jax.ShapeDtypeStruct((NT, D), BF16),
                   jax.ShapeDtypeStruct((NT, D), F32)),
        grid=(NT // TM,),
        in_specs=[
            row(),
            pl.BlockSpec((None, D, D), lambda i: (l, 0, 0)),
            row(),
            pl.BlockSpec((None, None, 6, D), lambda i: (l, midx(i), 0, 0)),
            pl.BlockSpec((None, 1, D), lambda i: (l, 0, 0)),
            pl.BlockSpec((None, 1, D), lambda i: (l, 0, 0)),
        ],
        out_specs=(row(), row(), row()),
        compiler_params=_cp(("parallel",)),
        name="outproj_norm1",
    )(mixp, w_out_bf, xs, mods, ln_g, ln_b)


def _topk_rows(x, tie, k, on_pick):
    for r in range(k):
        m = jnp.max(x, axis=0, keepdims=True)
        pos = jnp.min(jnp.where(x == m, tie, jnp.int32(2 ** 30)), axis=0, keepdims=True)
        sel = tie == pos
        on_pick(r, m, pos, sel)
        x = jnp.where(sel, -jnp.inf, x)


def _route_kernel(h_ref, wqt_ref, k1_ref, k2_ref, idx_ref, gate_ref, *, heads, dk, n_keys, topk):
    tm = h_ref.shape[0]
    nt = (((1,), (1,)), ((), ()))
    qt = lax.dot_general(wqt_ref[...], h_ref[...], nt, preferred_element_type=F32).astype(BF16)
    row_k = lax.broadcasted_iota(jnp.int32, (n_keys, tm), 0)
    row_t = lax.broadcasted_iota(jnp.int32, (topk, tm), 0)
    row_8 = lax.broadcasted_iota(jnp.int32, (SUBLANE, tm), 0)
    assert topk == 2 * SUBLANE

    def top_keys(s):
        out = {"v": jnp.zeros((topk, tm), F32), "i": jnp.zeros((topk, tm), jnp.int32)}

        def pick(r, m, pos, sel):
            out["v"] = jnp.where(row_t == r, m, out["v"])
            out["i"] = jnp.where(row_t == r, pos, out["i"])

        _topk_rows(s, row_k, topk, pick)
        return out["v"], out["i"]

    idx_rows, gate_rows = [], []
    for h in range(heads):
        s1 = jnp.dot(k1_ref[h], qt[(2 * h) * dk:(2 * h + 1) * dk, :], preferred_element_type=F32)
        s2 = jnp.dot(k2_ref[h], qt[(2 * h + 1) * dk:(2 * h + 2) * dk, :], preferred_element_type=F32)
        v1, i1 = top_keys(s1)
        v2, i2 = top_keys(s2)
        cv = [v1[0:1] + v2]
        ci = [i1[0:1] * n_keys + i2]
        cf = [row_t]
        for a in range(1, SUBLANE):
            ok = row_8 < topk // (a + 1)
            cv.append(jnp.where(ok, v1[a:a + 1] + v2[0:SUBLANE], -jnp.inf))
            ci.append(i1[a:a + 1] * n_keys + i2[0:SUBLANE])
            cf.append(a * topk + row_8)
        cv.append(v1[SUBLANE:] + v2[0:1])
        ci.append(i1[SUBLANE:] * n_keys + i2[0:1])
        cf.append((SUBLANE + row_8) * topk)
        cand_v, cand_i, cand_f = (jnp.concatenate(c, axis=0) for c in (cv, ci, cf))
        fin = {"v": jnp.zeros((topk, tm), F32), "i": jnp.zeros((topk, tm), jnp.int32)}

        def pick3(r, m, pos, sel, fin=fin, cand_i=cand_i):
            e = jnp.max(jnp.where(sel, cand_i, -1), axis=0, keepdims=True)
            fin["v"] = jnp.where(row_t == r, m, fin["v"])
            fin["i"] = jnp.where(row_t == r, e, fin["i"])

        _topk_rows(cand_v, cand_f, topk, pick3)
        ex = jnp.exp(fin["v"] - fin["v"][0:1])
        gate_rows.append(ex / jnp.sum(ex, axis=0, keepdims=True))
        idx_rows.append(fin["i"].astype(F32))
    idx_ref[...] = jnp.concatenate(idx_rows, axis=0).T.astype(jnp.int32)
    gate_ref[...] = jnp.concatenate(gate_rows, axis=0).T


def _route(h2b, wqt_bf, k1_bf, k2_bf, l, TM):
    NT, D = h2b.shape
    H, NK, DK = k1_bf.shape[1:]
    PQ = wqt_bf.shape[1]
    nsel = H * PEER_TOPK
    return pl.pallas_call(
        functools.partial(_route_kernel, heads=H, dk=DK, n_keys=NK, topk=PEER_TOPK),
        out_shape=(jax.ShapeDtypeStruct((NT, nsel), jnp.int32),
                   jax.ShapeDtypeStruct((NT, nsel), F32)),
        grid=(NT // TM,),
        in_specs=[
            pl.BlockSpec((TM, D), lambda i: (i, 0)),
            pl.BlockSpec((None, PQ, D), lambda i: (l, 0, 0)),
            pl.BlockSpec((None, H, NK, DK), lambda i: (l, 0, 0, 0)),
            pl.BlockSpec((None, H, NK, DK), lambda i: (l, 0, 0, 0)),
        ],
        out_specs=(pl.BlockSpec((TM, nsel), lambda i: (i, 0)),
                   pl.BlockSpec((TM, nsel), lambda i: (i, 0))),
        compiler_params=_cp(("parallel",)),
        name="peer_route",
    )(h2b, wqt_bf, k1_bf, k2_bf)


def _peer_kernel(idx_ref, h_ref, gate_ref, x1_ref, mod_ref, g_ref, b_ref, tab_ref,
                 o_ref, *scratch, tb, nsel, dj, nbuf, alpha):
    bufs, ffn_ref, sem = scratch[:nbuf], scratch[nbuf], scratch[nbuf + 1]

    def issue(t, s):
        for k in range(nsel):
            e = idx_ref[t, k]
            pltpu.make_async_copy(tab_ref.at[e], bufs[s].at[:, k, :], sem.at[s]).start()

    def wait(s):
        pltpu.make_async_copy(bufs[s], bufs[s], sem.at[s]).wait()

    eye = (lax.broadcasted_iota(jnp.int32, (nsel, nsel), 0)
           == lax.broadcasted_iota(jnp.int32, (nsel, nsel), 1))
    hi_mask = jnp.uint32(0xFFFF0000)

    def compute(t, s):
        hrow = h_ref[pl.ds(t, 1), :]
        acc = jnp.zeros((nsel, LANE), F32)
        for j in range(dj):
            u = lax.bitcast_convert_type(bufs[s][j] << 16, F32)
            acc = acc + u * hrow[:, j * LANE:(j + 1) * LANE]
        sc = jnp.sum(acc, axis=-1, keepdims=True)
        gcol = jnp.sum(jnp.where(eye, gate_ref[pl.ds(t, 1), :], 0.0), axis=-1, keepdims=True)
        act = _gelu(sc) * gcol
        out = []
        for j in range(dj):
            v = lax.bitcast_convert_type(bufs[s][j] & hi_mask, F32)
            out.append(jnp.sum(v * act, axis=0, keepdims=True))
        return jnp.concatenate(out, axis=-1)

    assert nbuf == SUBLANE and tb % nbuf == 0
    ahead = nbuf - 1
    n_groups = tb // nbuf

    def group(g, last):
        base = pl.multiple_of(g * nbuf, nbuf)
        rows = []
        for s in range(nbuf):
            wait(s)
            if not last or s == 0:
                issue(base + s + ahead, (s + ahead) % nbuf)
            rows.append(compute(base + s, s))
        ffn_ref[pl.ds(base, nbuf), :] = jnp.concatenate(rows, axis=0)

    for t in range(ahead):
        issue(t, t)

    def steady(g, carry):
        group(g, False)
        return carry

    lax.fori_loop(0, n_groups - 1, steady, 0)
    group(n_groups - 1, True)
    x2 = _ln(alpha * x1_ref[...] + mod_ref[5:6, :] * ffn_ref[...]) * g_ref[...] + b_ref[...]
    o_ref[...] = x2


def _peer(idx, h2r, gates, x1, mods, ln_g, ln_b, table, l, midx, alpha, TB):
    NT, D = x1.shape
    nsel = idx.shape[1]
    DJ = D // LANE
    nbuf = 8
    row = lambda: pl.BlockSpec((TB, D), lambda i: (i, 0))
    return pl.pallas_call(
        functools.partial(_peer_kernel, tb=TB, nsel=nsel, dj=DJ, nbuf=nbuf, alpha=alpha),
        out_shape=jax.ShapeDtypeStruct((NT, D), F32),
        grid=(NT // TB,),
        in_specs=[
            pl.BlockSpec((TB, nsel), lambda i: (i, 0), memory_space=pltpu.SMEM),
            row(),
            pl.BlockSpec((TB, nsel), lambda i: (i, 0)),
            row(),
            pl.BlockSpec((None, None, 6, D), lambda i: (l, midx(i), 0, 0)),
            pl.BlockSpec((None, 1, D), lambda i: (l, 0, 0)),
            pl.BlockSpec((None, 1, D), lambda i: (l, 0, 0)),
            pl.BlockSpec(memory_space=pl.ANY),
        ],
        out_specs=row(),
        scratch_shapes=[pltpu.VMEM((DJ, nsel, LANE), jnp.uint32) for _ in range(nbuf)]
        + [pltpu.VMEM((TB, D), F32), pltpu.SemaphoreType.DMA((nbuf,))],
        compiler_params=_cp(("arbitrary",)),
        name="peer_experts",
    )(idx, h2r, gates, x1, mods, ln_g, ln_b, table)


def _pack_table(u, v):
    E, D = u.shape
    ub = lax.bitcast_convert_type(u.astype(BF16), jnp.uint16).astype(jnp.uint32)
    vb = lax.bitcast_convert_type(v.astype(BF16), jnp.uint16).astype(jnp.uint32)
    return (ub | (vb << 16)).reshape(E, D // LANE, LANE)


def _rope_tables(S, TM):
    rows = S // GRID_W
    axis_dim = LANE // 2
    row = jnp.repeat(jnp.arange(rows, dtype=F32), GRID_W)
    col = jnp.tile(jnp.arange(GRID_W, dtype=F32), rows)
    inv_freq = ROPE_THETA ** (-jnp.arange(0, axis_dim, 2, dtype=F32) / axis_dim)
    ar = row[:, None] * inv_freq
    ac = col[:, None] * inv_freq
    cos = jnp.concatenate([jnp.cos(ar), jnp.cos(ar), jnp.cos(ac), jnp.cos(ac)], axis=1)
    sin = jnp.concatenate([-jnp.sin(ar), jnp.sin(ar), -jnp.sin(ac), jnp.sin(ac)], axis=1)
    cos = jnp.concatenate([cos, jnp.ones((TM, LANE), F32)], axis=0)
    sin = jnp.concatenate([sin, jnp.zeros((TM, LANE), F32)], axis=0)
    return cos, sin


def kernel(x, c, ctx, c_ctx, w_ada, b_ada, w_in, q_gain, k_gain, conv_w, conv_b, wa_f, ba_f, wx_f, bx_f, lam_f, wa_b, ba_b, wx_b, bx_b, lam_b, w_attn_br, w_lru_br, w_out, ln1_g, ln1_b, w_pq, sub_k1, sub_k2, peer_u, peer_v, ln2_g, ln2_b):
    B, S, D = x.shape
    C = ctx.shape[1]
    L = w_in.shape[0]
    HD = q_gain.shape[-1]
    W = conv_w.shape[-1]
    ATTN_W = w_attn_br.shape[1]
    IN_COLS = w_in.shape[-1]
    KV_W = (IN_COLS - 2 * W - ATTN_W - 2 * D) // 2
    KVH = KV_W // HD
    dims = dict(B=B, S=S, C=C, D=D, HD=HD, W=W, ATTN_W=ATTN_W, KV_W=KV_W, KVH=KVH,
                G=(ATTN_W // HD) // KVH, CTX_COLS=2 * KV_W + W,
                Q_END=2 * KV_W + W + ATTN_W, XG_END=2 * KV_W + 2 * W + ATTN_W)
    alpha = float((2 * L) ** 0.25)
    NT = B * S + B * C

    TM = _tile(C, 256)
    TBIG = _tile(math.gcd(S, B * C), 512)
    TB = _tile(C, 128)

    def midx_for(tm):
        n_lat, per_b = (B * S) // tm, S // tm
        return lambda i: jnp.where(i < n_lat, i // per_b, B)

    MP = -(-(B + 1) // SUBLANE) * SUBLANE
    cond = jnp.zeros((MP, D), F32).at[:B].set(c).at[B].set(c_ctx)
    mods = _ada_mods(cond, w_ada, b_ada).reshape(L, MP, 6, D)

    xs = jnp.concatenate([x.reshape(B * S, D), ctx.reshape(B * C, D)], axis=0)
    cos_t, sin_t = _rope_tables(S, TM)

    w_in_bf = w_in.astype(BF16)
    wa_bf, wl_bf, wo_bf = (w.astype(BF16) for w in (w_attn_br, w_lru_br, w_out))
    wqt_bf = jnp.swapaxes(w_pq, 1, 2).astype(BF16)
    k1_bf, k2_bf = sub_k1.astype(BF16), sub_k2.astype(BF16)
    vec3 = lambda a: a.reshape(L, 1, a.shape[-1])
    lp = dict(conv_w=conv_w, conv_b=vec3(conv_b),
              wa_f=wa_f.astype(BF16), ba_f=vec3(ba_f.reshape(L, W)),
              wx_f=wx_f.astype(BF16), bx_f=vec3(bx_f.reshape(L, W)), lam_f=vec3(lam_f),
              wa_b=wa_b.astype(BF16), ba_b=vec3(ba_b.reshape(L, W)),
              wx_b=wx_b.astype(BF16), bx_b=vec3(bx_b.reshape(L, W)), lam_b=vec3(lam_b))
    qg, kg = vec3(q_gain), vec3(k_gain)
    g1, b1, g2, b2 = vec3(ln1_g), vec3(ln1_b), vec3(ln2_g), vec3(ln2_b)
    zeros_state = jnp.zeros((B, 1, W), F32)
    y = jnp.zeros((NT, W), BF16)

    for l in range(L):
        px = _inproj(xs, mods, w_in_bf, l, midx_for(TBIG), TBIG)
        kn, vb, qn = _qkprep(px, cos_t, sin_t, kg, qg, dims, l, TM)
        attn_o = _attention(qn, kn, vb, dims, TM)
        y, hcf, hcb = _lru(px, y, zeros_state, zeros_state, lp, dims, l, C, (B * S) // C)
        y, _, _ = _lru(px, y, hcf, hcb, lp, dims, l, S, 0)
        mixp = _merge(attn_o, y, px, wa_bf, wl_bf, dims, l, TBIG)
        x1, h2b, h2r = _outproj(mixp, wo_bf, xs, mods, g1, b1, l, midx_for(TM), alpha, TM)
        idx, gates = _route(h2b, wqt_bf, k1_bf, k2_bf, l, TM)
        table = _pack_table(peer_u[l], peer_v[l])
        xs = _peer(idx, h2r, gates, x1, mods, g2, b2, table, l, midx_for(TB), alpha, TB)
    return xs[:B * S].reshape(B, S, D)
```
